```python
import math
import jax, jax.numpy as jnp
from jax import lax
import numpy as np

D_MODEL = 1024
BATCH = 2
SEQ = 8192
DEPTH = 1
DEC_BATCH = 128
DEC_SEQ = 8
PAST_LEN = 8192
PAGE_SIZE = 128

SSM_HEADS = 8
SSM_HEAD_DIM = 64
D_SSM = SSM_HEADS * SSM_HEAD_DIM
SSM_GROUPS = 2
HEADS_PER_GROUP = SSM_HEADS // SSM_GROUPS
SSM_STATE = 128
CONV_WIDTH = 4
CONV_DIM = D_SSM + 2 * SSM_GROUPS * SSM_STATE
SSD_CHUNK = 128
DT_MIN = 0.001
DT_MAX = 0.1

MLA_HEADS = 8
QK_NOPE = 64
QK_ROPE = 32
V_HEAD = 64
D_MLA = MLA_HEADS * V_HEAD
Q_LORA = 384
KV_LORA = 256
ROPE_THETA = 10000.0
Q_BLOCK = 128
ATTN_SCALE = (QK_NOPE + QK_ROPE) ** -0.5

D_MIX = D_SSM + D_MLA
IN_SPLITS = (D_SSM, D_SSM + CONV_DIM, D_SSM + CONV_DIM + SSM_HEADS, D_SSM + CONV_DIM + SSM_HEADS + Q_LORA, D_SSM + CONV_DIM + SSM_HEADS + Q_LORA + KV_LORA)
IN_COLS = IN_SPLITS[-1] + QK_ROPE

N_EXPERTS = 32
TOP_K = 4
D_EXPERT = D_MODEL
SWIGLU_LIMIT = 7.0
SWIGLU_ALPHA = 1.702
MOE_BLOCK = 128

EPS = 1e-6
N_ADA = 6

kernel_name = 'hymba_ssd_mla_moe_adaln_step'


def rmsnorm(x, g):
    xf = x.astype(jnp.float32)
    y = xf * lax.rsqrt(jnp.mean(xf * xf, axis=-1, keepdims=True) + EPS)
    return (y * g.astype(jnp.float32)).astype(x.dtype)


def rope_angles(pos):
    inv = ROPE_THETA ** (-jnp.arange(0, QK_ROPE, 2, dtype=jnp.float32) / QK_ROPE)
    ang = pos.astype(jnp.float32)[:, None] * inv[None, :]
    return jnp.cos(ang), jnp.sin(ang)


def apply_rope(x, cos, sin):
    shape = (cos.shape[0],) + (1,) * (x.ndim - 3) + (cos.shape[1],)
    cos, sin = cos.reshape(shape), sin.reshape(shape)
    xf = x.astype(jnp.float32)
    x1, x2 = xf[..., :QK_ROPE // 2], xf[..., QK_ROPE // 2:]
    return jnp.concatenate([x1 * cos - x2 * sin, x2 * cos + x1 * sin], axis=-1).astype(x.dtype)


def causal_conv(xbc, conv_prev, w, b):
    T = xbc.shape[1]
    xp = jnp.concatenate([conv_prev.astype(xbc.dtype), xbc], axis=1)
    y = b
    for k in range(CONV_WIDTH):
        y = y + xp[:, k:k + T] * w[k]
    return jax.nn.silu(y), xp[:, xp.shape[1] - (CONV_WIDTH - 1):]


def ssd_scan(x, dt, a, bm, cm, s0):
    nb, T = x.shape[0], x.shape[1]
    q = min(SSD_CHUNK, T)
    nc = -(-T // q)
    pad = nc * q - T
    if pad:
        padw = lambda t: jnp.pad(t, ((0, 0), (0, pad)) + ((0, 0),) * (t.ndim - 2))
        x, dt, bm, cm = padw(x), padw(dt), padw(bm), padw(cm)
    chunk = lambda t: t.reshape((nb, nc, q) + t.shape[2:])
    xdt = chunk(x.astype(jnp.float32) * dt[..., None])
    acum = jnp.cumsum(chunk(dt * a), axis=2)
    bc = chunk(bm.astype(jnp.float32))
    cc = chunk(cm.astype(jnp.float32))
    seg = acum[:, :, :, None] - acum[:, :, None, :]
    causal = jnp.tril(jnp.ones((q, q), bool))[None, None, :, :, None, None]
    lmat = jnp.exp(jnp.where(causal, seg, -jnp.inf))
    cb = jnp.einsum('bcqgn,bcsgn->bcqsg', cc, bc)
    y_diag = jnp.einsum('bcqsg,bcqsgj,bcsgjp->bcqgjp', cb, lmat, xdt)
    decay_end = jnp.exp(acum[:, :, -1:] - acum)
    chunk_states = jnp.einsum('bcsgn,bcsgj,bcsgjp->bcgjpn', bc, decay_end, xdt)
    chunk_decay = jnp.exp(acum[:, :, -1])

    def step(s, inp):
        dec, st = inp
        return dec[..., None, None] * s + st, s

    s_final, s_prev = lax.scan(step, s0, (jnp.moveaxis(chunk_decay, 1, 0), jnp.moveaxis(chunk_states, 1, 0)))
    s_prev = jnp.moveaxis(s_prev, 0, 1)
    y_off = jnp.einsum('bcqgn,bcgjpn,bcqgj->bcqgjp', cc, s_prev, jnp.exp(acum))
    y = (y_diag + y_off).reshape((nb, nc * q) + x.shape[2:])[:, :T]
    return y, s_final


def mla_prompt(q_nope, q_rope, lat, k_rope, w_uk, w_uv):
    nb, T = lat.shape[0], lat.shape[1]
    blk = min(Q_BLOCK, T)
    n_blk = T // blk
    k_nope = jnp.einsum('btc,chd->bthd', lat, w_uk)
    v = jnp.einsum('btc,chv->bthv', lat, w_uv)
    qn = jnp.moveaxis(q_nope.reshape(nb, n_blk, blk, MLA_HEADS, QK_NOPE), 1, 0)
    qr = jnp.moveaxis(q_rope.reshape(nb, n_blk, blk, MLA_HEADS, QK_ROPE), 1, 0)
    kpos = jnp.arange(T)

    def block(args):
        qn_b, qr_b, i = args
        s = (jnp.einsum('bqhd,bthd->bhqt', qn_b, k_nope) + jnp.einsum('bqhr,btr->bhqt', qr_b, k_rope)).astype(jnp.float32) * ATTN_SCALE
        qpos = i * blk + jnp.arange(blk)
        s = jnp.where(kpos[None, :] <= qpos[:, None], s, -jnp.inf)
        pr = jax.nn.softmax(s, axis=-1).astype(v.dtype)
        return jnp.einsum('bhqt,bthv->bqhv', pr, v)

    o = lax.map(block, (qn, qr, jnp.arange(n_blk)))
    return jnp.moveaxis(o, 0, 1).reshape(nb, T, D_MLA)


def mla_sample(q_nope, q_rope, lat, k_rope, past_lat, past_rope, w_uk, w_uv):
    past, T = past_lat.shape[1], lat.shape[1]
    keys_lat = jnp.concatenate([past_lat.astype(lat.dtype), lat], axis=1)
    keys_rope = jnp.concatenate([past_rope.astype(k_rope.dtype), k_rope], axis=1)
    q_lat = jnp.einsum('bqhd,chd->bqhc', q_nope, w_uk)
    s = (jnp.einsum('bqhc,btc->bhqt', q_lat, keys_lat) + jnp.einsum('bqhr,btr->bhqt', q_rope, keys_rope)).astype(jnp.float32) * ATTN_SCALE
    kpos = jnp.arange(past + T)
    qpos = past + jnp.arange(T)
    s = jnp.where(kpos[None, :] <= qpos[:, None], s, -jnp.inf)
    pr = jax.nn.softmax(s, axis=-1).astype(keys_lat.dtype)
    o_lat = jnp.einsum('bhqt,btc->bqhc', pr, keys_lat)
    o = jnp.einsum('bqhc,chv->bqhv', o_lat, w_uv)
    return o.reshape(o.shape[0], T, D_MLA)


def token_mixers(h, pos, conv_prev, ssm_prev, past_lat, past_rope, p):
    nb, T = h.shape[0], h.shape[1]
    proj = h @ p['w_in']
    z, xbc, dt_raw, cq, ckv, kr = jnp.split(proj, IN_SPLITS, axis=-1)
    xbc, conv_new = causal_conv(xbc, conv_prev, p['conv_w'], p['conv_b'])
    xs, bm, cm = jnp.split(xbc, (D_SSM, D_SSM + SSM_GROUPS * SSM_STATE), axis=-1)
    xs = xs.reshape(nb, T, SSM_GROUPS, HEADS_PER_GROUP, SSM_HEAD_DIM)
    bm = bm.reshape(nb, T, SSM_GROUPS, SSM_STATE)
    cm = cm.reshape(nb, T, SSM_GROUPS, SSM_STATE)
    dt = jax.nn.softplus(dt_raw.astype(jnp.float32) + p['dt_bias'].astype(jnp.float32)).reshape(nb, T, SSM_GROUPS, HEADS_PER_GROUP)
    a = -jnp.exp(p['a_log'].astype(jnp.float32)).reshape(SSM_GROUPS, HEADS_PER_GROUP)
    s0 = ssm_prev.astype(jnp.float32).reshape(nb, SSM_GROUPS, HEADS_PER_GROUP, SSM_HEAD_DIM, SSM_STATE)
    y, s_new = ssd_scan(xs, dt, a, bm, cm, s0)
    y = y + p['d_skip'].astype(jnp.float32).reshape(SSM_GROUPS, HEADS_PER_GROUP)[..., None] * xs.astype(jnp.float32)
    y = y.reshape(nb, T, D_SSM) * jax.nn.silu(z.astype(jnp.float32))
    y_ssm = rmsnorm(y, p['g_ssm']).astype(h.dtype)
    cos, sin = rope_angles(pos)
    q = jnp.einsum('btr,rhe->bthe', rmsnorm(cq, p['g_q']), p['w_uq'])
    q_nope = q[..., :QK_NOPE]
    q_rope = apply_rope(q[..., QK_NOPE:], cos, sin)
    lat = rmsnorm(ckv, p['g_kv'])
    k_rope = apply_rope(kr, cos, sin)
    if past_lat is None:
        y_mla = mla_prompt(q_nope, q_rope, lat, k_rope, p['w_uk'], p['w_uv'])
    else:
        y_mla = mla_sample(q_nope, q_rope, lat, k_rope, past_lat, past_rope, p['w_uk'], p['w_uv'])
    out = jnp.concatenate([y_ssm, y_mla.astype(h.dtype)], axis=-1) @ p['w_o']
    return out, (lat, k_rope, s_new.reshape(nb, SSM_HEADS, SSM_HEAD_DIM, SSM_STATE), conv_new)


def moe(h, p):
    shp = h.shape
    xt = h.reshape(-1, D_MODEL)
    n = xt.shape[0]
    logits = (xt @ p['w_router'] + p['b_router']).astype(jnp.float32)
    top_v, top_e = lax.top_k(logits, TOP_K)
    gates = jax.nn.softmax(top_v, axis=-1)
    flat_e = top_e.reshape(-1)
    flat_g = gates.reshape(-1)
    flat_tok = jnp.repeat(jnp.arange(n, dtype=jnp.int32), TOP_K)
    order = jnp.argsort(flat_e)
    e_sorted = flat_e[order]
    counts = jnp.bincount(flat_e, length=N_EXPERTS)
    padded = (counts + MOE_BLOCK - 1) // MOE_BLOCK * MOE_BLOCK
    pad_end = jnp.cumsum(padded)
    pad_start = pad_end - padded
    start = jnp.cumsum(counts) - counts
    dest = pad_start[e_sorted] + jnp.arange(n * TOP_K) - start[e_sorted]
    n_blocks = -(-(n * TOP_K) // MOE_BLOCK) + N_EXPERTS
    cap = n_blocks * MOE_BLOCK
    slot_tok = jnp.full((cap,), n, jnp.int32).at[dest].set(flat_tok[order])
    slot_gate = jnp.zeros((cap,), jnp.float32).at[dest].set(flat_g[order])
    block_e = jnp.minimum(jnp.searchsorted(pad_end, jnp.arange(n_blocks) * MOE_BLOCK, side='right'), N_EXPERTS - 1)
    x_pad = jnp.concatenate([xt, jnp.zeros((1, D_MODEL), xt.dtype)], axis=0)
    xb = x_pad[slot_tok].reshape(n_blocks, MOE_BLOCK, D_MODEL)

    def expert_block(args):
        xblk, e = args
        gu = xblk @ p['w_gate_up'][e] + p['b_gate_up'][e]
        gate, up = gu[:, :D_EXPERT], gu[:, D_EXPERT:]
        gate = jnp.minimum(gate, SWIGLU_LIMIT)
        up = jnp.clip(up, -SWIGLU_LIMIT, SWIGLU_LIMIT)
        act = (up + 1) * (gate * jax.nn.sigmoid(SWIGLU_ALPHA * gate))
        return act @ p['w_down'][e] + p['b_down'][e]

    yb = lax.map(expert_block, (xb, block_e)).reshape(cap, D_MODEL)
    y = jnp.zeros((n + 1, D_MODEL), jnp.float32).at[slot_tok].add(yb.astype(jnp.float32) * slot_gate[:, None])[:n]
    return y.astype(h.dtype).reshape(shp)


def layer(x, c, pos, conv_prev, ssm_prev, past_lat, past_rope, p):
    ada = (c @ p['w_ada'] + p['b_ada'])[:, None, :]
    sh1, sc1, g1, sh2, sc2, g2 = jnp.split(ada, N_ADA, axis=-1)
    h = rmsnorm(x, p['g_mix']) * (1 + sc1) + sh1
    mix, new_state = token_mixers(h, pos, conv_prev, ssm_prev, past_lat, past_rope, p)
    x = x + g1 * mix
    h = rmsnorm(x, p['g_ffn']) * (1 + sc2) + sh2
    x = x + g2 * moe(h, p)
    return x, new_state


def setup_inputs(seed: int = 0) -> dict:
    key = jax.random.key(seed)
    ks = iter(jax.random.split(key, 48))
    nrm = lambda shape, scale: jax.random.normal(next(ks), shape, jnp.float32) * scale
    n_pages = PAST_LEN // PAGE_SIZE
    n_used = DEC_BATCH * n_pages
    n_phys = n_used + (n_used + 3) // 4
    x_prompt = nrm((BATCH, SEQ, D_MODEL), 1.0)
    x_sample = nrm((DEC_BATCH, DEC_SEQ, D_MODEL), 1.0)
    c_prompt = nrm((BATCH, D_MODEL), 1.0)
    c_sample = nrm((DEC_BATCH, D_MODEL), 1.0)
    cache_kv_latent = nrm((DEPTH, n_phys, PAGE_SIZE, KV_LORA), 1.0)
    cache_k_rope = nrm((DEPTH, n_phys, PAGE_SIZE, QK_ROPE), 1.0)
    page_table = jax.random.permutation(next(ks), n_phys)[:n_used].reshape(DEC_BATCH, n_pages).astype(jnp.int32)
    state_ssm = nrm((DEPTH, DEC_BATCH, SSM_HEADS, SSM_HEAD_DIM, SSM_STATE), 0.1)
    state_conv = nrm((DEPTH, DEC_BATCH, CONV_WIDTH - 1, CONV_DIM), 1.0)
    w_ada = nrm((DEPTH, D_MODEL, N_ADA * D_MODEL), 0.5 * D_MODEL ** -0.5)
    b_ada = nrm((DEPTH, N_ADA * D_MODEL), 0.01)
    g_mix = 1.0 + nrm((DEPTH, D_MODEL), 0.02)
    g_ffn = 1.0 + nrm((DEPTH, D_MODEL), 0.02)
    w_in = nrm((DEPTH, D_MODEL, IN_COLS), D_MODEL ** -0.5)
    conv_w = nrm((DEPTH, CONV_WIDTH, CONV_DIM), CONV_WIDTH ** -0.5)
    conv_b = nrm((DEPTH, CONV_DIM), 0.01)
    dt0 = jnp.exp(jax.random.uniform(next(ks), (DEPTH, SSM_HEADS), jnp.float32, math.log(DT_MIN), math.log(DT_MAX)))
    dt_bias = dt0 + jnp.log(-jnp.expm1(-dt0))
    a_log = jnp.log(jax.random.uniform(next(ks), (DEPTH, SSM_HEADS), jnp.float32, 1.0, 16.0))
    d_skip = 1.0 + nrm((DEPTH, SSM_HEADS), 0.02)
    g_ssm = 1.0 + nrm((DEPTH, D_SSM), 0.02)
    g_q = 1.0 + nrm((DEPTH, Q_LORA), 0.02)
    w_uq = nrm((DEPTH, Q_LORA, MLA_HEADS, QK_NOPE + QK_ROPE), Q_LORA ** -0.5)
    g_kv = 1.0 + nrm((DEPTH, KV_LORA), 0.02)
    w_uk = nrm((DEPTH, KV_LORA, MLA_HEADS, QK_NOPE), KV_LORA ** -0.5)
    w_uv = nrm((DEPTH, KV_LORA, MLA_HEADS, V_HEAD), KV_LORA ** -0.5)
    w_o = nrm((DEPTH, D_MIX, D_MODEL), D_MIX ** -0.5)
    w_router = nrm((DEPTH, D_MODEL, N_EXPERTS), D_MODEL ** -0.5)
    b_router = nrm((DEPTH, N_EXPERTS), 0.01)
    w_gate_up = nrm((DEPTH, N_EXPERTS, D_MODEL, 2 * D_EXPERT), D_MODEL ** -0.5)
    b_gate_up = nrm((DEPTH, N_EXPERTS, 2 * D_EXPERT), 0.01)
    w_down = nrm((DEPTH, N_EXPERTS, D_EXPERT, D_MODEL), D_EXPERT ** -0.5)
    b_down = nrm((DEPTH, N_EXPERTS, D_MODEL), 0.01)
    g_final = 1.0 + nrm((D_MODEL,), 0.02)
    return {'x_prompt': x_prompt, 'x_sample': x_sample, 'c_prompt': c_prompt, 'c_sample': c_sample,
            'cache_kv_latent': cache_kv_latent, 'cache_k_rope': cache_k_rope, 'page_table': page_table,
            'state_ssm': state_ssm, 'state_conv': state_conv,
            'w_ada': w_ada, 'b_ada': b_ada, 'g_mix': g_mix, 'g_ffn': g_ffn, 'w_in': w_in,
            'conv_w': conv_w, 'conv_b': conv_b, 'dt_bias': dt_bias, 'a_log': a_log, 'd_skip': d_skip, 'g_ssm': g_ssm,
            'g_q': g_q, 'w_uq': w_uq, 'g_kv': g_kv, 'w_uk': w_uk, 'w_uv': w_uv, 'w_o': w_o,
            'w_router': w_router, 'b_router': b_router, 'w_gate_up': w_gate_up, 'b_gate_up': b_gate_up,
            'w_down': w_down, 'b_down': b_down, 'g_final': g_final}


def reference(x_prompt, x_sample, c_prompt, c_sample, cache_kv_latent, cache_k_rope, page_table, state_ssm, state_conv,
              w_ada, b_ada, g_mix, g_ffn, w_in, conv_w, conv_b, dt_bias, a_log, d_skip, g_ssm,
              g_q, w_uq, g_kv, w_uk, w_uv, w_o, w_router, b_router, w_gate_up, b_gate_up, w_down, b_down, g_final):
    bp, tp = x_prompt.shape[0], x_prompt.shape[1]
    bd, td = x_sample.shape[0], x_sample.shape[1]
    past = page_table.shape[1] * cache_kv_latent.shape[2]
    pos_p = jnp.arange(tp)
    pos_d = past + jnp.arange(td)
    xp, xd = x_prompt, x_sample
    lat_p, rope_p, ssm_p, conv_p = [], [], [], []
    lat_s, rope_s, ssm_s, conv_s = [], [], [], []
    for l in range(DEPTH):
        p = {'w_ada': w_ada[l], 'b_ada': b_ada[l], 'g_mix': g_mix[l], 'g_ffn': g_ffn[l], 'w_in': w_in[l],
             'conv_w': conv_w[l], 'conv_b': conv_b[l], 'dt_bias': dt_bias[l], 'a_log': a_log[l], 'd_skip': d_skip[l],
             'g_ssm': g_ssm[l], 'g_q': g_q[l], 'w_uq': w_uq[l], 'g_kv': g_kv[l], 'w_uk': w_uk[l], 'w_uv': w_uv[l],
             'w_o': w_o[l], 'w_router': w_router[l], 'b_router': b_router[l], 'w_gate_up': w_gate_up[l],
             'b_gate_up': b_gate_up[l], 'w_down': w_down[l], 'b_down': b_down[l]}
        conv0 = jnp.zeros((bp, CONV_WIDTH - 1, CONV_DIM), xp.dtype)
        ssm0 = jnp.zeros((bp, SSM_HEADS, SSM_HEAD_DIM, SSM_STATE), jnp.float32)
        xp, (la, kr, ss, cv) = layer(xp, c_prompt, pos_p, conv0, ssm0, None, None, p)
        lat_p.append(la); rope_p.append(kr); ssm_p.append(ss.astype(state_ssm.dtype)); conv_p.append(cv)
        past_lat = cache_kv_latent[l][page_table].reshape(bd, past, KV_LORA)
        past_rope = cache_k_rope[l][page_table].reshape(bd, past, QK_ROPE)
        xd, (la, kr, ss, cv) = layer(xd, c_sample, pos_d, state_conv[l], state_ssm[l], past_lat, past_rope, p)
        lat_s.append(la); rope_s.append(kr); ssm_s.append(ss.astype(state_ssm.dtype)); conv_s.append(cv)
    y_prompt = rmsnorm(xp, g_final)
    y_sample = rmsnorm(xd, g_final)
    return (y_prompt, y_sample, jnp.stack(lat_p), jnp.stack(rope_p), jnp.stack(ssm_p), jnp.stack(conv_p), jnp.stack(lat_s), jnp.stack(rope_s), jnp.stack(ssm_s), jnp.stack(conv_s))
```

```python
import functools
import math

import jax
import jax.numpy as jnp
from jax import lax
from jax.experimental import pallas as pl
from jax.experimental.pallas import tpu as pltpu

F32 = jnp.float32
BF16 = jnp.bfloat16
I32 = jnp.int32
HIGHEST = lax.Precision.HIGHEST

D_MODEL = 1024
SSM_HEADS = 8
SSM_HEAD_DIM = 64
D_SSM = SSM_HEADS * SSM_HEAD_DIM
SSM_GROUPS = 2
HEADS_PER_GROUP = SSM_HEADS // SSM_GROUPS
SSM_STATE = 128
CONV_WIDTH = 4
CONV_DIM = D_SSM + 2 * SSM_GROUPS * SSM_STATE
SSD_CHUNK = 128
MLA_HEADS = 8
QK_NOPE = 64
QK_ROPE = 32
V_HEAD = 64
D_MLA = MLA_HEADS * V_HEAD
Q_LORA = 384
KV_LORA = 256
ROPE_THETA = 10000.0
ATTN_SCALE = (QK_NOPE + QK_ROPE) ** -0.5
N_EXPERTS = 32
TOP_K = 4
D_EXPERT = D_MODEL
SWIGLU_LIMIT = 7.0
SWIGLU_ALPHA = 1.702
EPS = 1e-6
N_ADA = 6

LANES = 128
SUBLANES = 8
HEAD_PAD = 128
MISC_DT = 64
NEG_BIG = -1e30

_C_Z = 0
_C_XBC = _C_Z + D_SSM
_C_CQ = _C_XBC + CONV_DIM
_C_CKV = _C_CQ + Q_LORA
_C_MISC = _C_CKV + KV_LORA
_C_END = _C_MISC + LANES

VMEM_LIMIT = 56 * 1024 * 1024


def _cparams(sem):
    return pltpu.CompilerParams(dimension_semantics=sem, vmem_limit_bytes=VMEM_LIMIT)


def _dot(a, b, precision=None):
    return jnp.dot(a, b, preferred_element_type=F32, precision=precision)


def _dot_nt(a, b, precision=None):
    return lax.dot_general(a, b, (((1,), (1,)), ((), ())), preferred_element_type=F32, precision=precision)


def _dot_tn(a, b):
    return lax.dot_general(a, b, (((0,), (0,)), ((), ())), preferred_element_type=F32)


def _rms(x, g):
    return x * lax.rsqrt(jnp.mean(x * x, axis=-1, keepdims=True) + EPS) * g


def _ada_kernel(c_ref, w_ref, b_ref, o_ref):
    o_ref[...] = _dot(c_ref[...], w_ref[...], HIGHEST) + b_ref[...]


def _ada(c, w, b):
    m, k = c.shape
    n = w.shape[1]
    tn = 512
    return pl.pallas_call(
        _ada_kernel,
        grid=(n // tn,),
        in_specs=[pl.BlockSpec((m, k), lambda j: (0, 0)),
                  pl.BlockSpec((k, tn), lambda j: (0, j)),
                  pl.BlockSpec((1, tn), lambda j: (0, j))],
        out_specs=pl.BlockSpec((m, tn), lambda j: (0, j)),
        out_shape=jax.ShapeDtypeStruct((m, n), F32),
        compiler_params=_cparams(("arbitrary",)),
    )(c, w, b.reshape(1, n))


def _in_kernel(*refs, prompt):
    (x_ref, sh_ref, sc_ref, gmix_ref, win_ref, wdt_ref, gq_ref, wq_ref, wqs_ref, gkv_ref,
     cosq_ref, sinq_ref, tabk_ref) = refs[:13]
    if prompt:
        wuk_ref, e2_ref, wuv_ref = refs[13:16]
        z_ref, xbc_ref, misc_ref, dtt_ref, q_ref, lat_ref, kr_ref, kf_ref, v_ref = refs[16:]
    else:
        z_ref, xbc_ref, misc_ref, dtt_ref, q_ref, lat_ref, kr_ref = refs[13:]

    x = x_ref[...]
    h = _rms(x, gmix_ref[...]) * (1.0 + sc_ref[...]) + sh_ref[...]
    hb = h.astype(BF16)
    z_ref[...] = _dot(hb, win_ref[:, _C_Z:_C_XBC])
    xbc_ref[...] = _dot(hb, win_ref[:, _C_XBC:_C_CQ])
    cq = _dot(hb, win_ref[:, _C_CQ:_C_CKV])
    ckv = _dot(hb, win_ref[:, _C_CKV:_C_MISC])
    misc = _dot(hb, win_ref[:, _C_MISC:_C_END])
    misc_ref[...] = misc
    dtt_ref[...] = _dot_nt(wdt_ref[...], hb)

    cqn = _rms(cq, gq_ref[...]).astype(BF16)
    cosq = jnp.concatenate([cosq_ref[...]] * MLA_HEADS, axis=1)
    sinq = jnp.concatenate([sinq_ref[...]] * MLA_HEADS, axis=1)
    q = _dot(cqn, wq_ref[...]) * cosq + _dot(cqn, wqs_ref[...]) * sinq
    q_ref[...] = q.astype(q_ref.dtype)

    lat = _rms(ckv, gkv_ref[...])
    lat_ref[...] = lat
    prod = misc * tabk_ref[...]
    krope = prod + pltpu.roll(prod, LANES - QK_ROPE, axis=1)
    kr_ref[...] = krope[:, :QK_ROPE]
    if prompt:
        latb = lat.astype(BF16)
        kf = _dot(latb, wuk_ref[...]) + _dot(krope.astype(BF16), e2_ref[...])
        kf_ref[...] = kf.astype(BF16)
        v_ref[...] = _dot(latb, wuv_ref[...]).astype(BF16)


def _in_proj(x2d, sh, sc, tabs, wts, *, prompt, tm):
    n = x2d.shape[0]
    g, r, _ = sh.shape
    tiles_per_group = (n // g) // tm
    cosq, sinq, tabk = tabs
    tab_tiles = cosq.shape[0] // tm
    full = lambda a: pl.BlockSpec(a.shape, lambda i: (0,) * a.ndim)
    row = lambda w: pl.BlockSpec((tm, w), lambda i: (i, 0))
    mod = pl.BlockSpec((None, r, D_MODEL), lambda i: (i // tiles_per_group, 0, 0))
    tab = pl.BlockSpec((tm, LANES), lambda i: (i % tab_tiles, 0))
    names = ["g_mix", "win", "wdt", "g_q", "wq", "wqs", "g_kv"]
    ins = [x2d, sh, sc] + [wts[k] for k in names] + [cosq, sinq, tabk]
    specs = [row(D_MODEL), mod, mod] + [full(wts[k]) for k in names] + [tab, tab, tab]
    qdt = BF16 if prompt else F32
    outs = [jax.ShapeDtypeStruct((n, D_SSM), F32), jax.ShapeDtypeStruct((n, CONV_DIM), F32),
            jax.ShapeDtypeStruct((n, LANES), F32), jax.ShapeDtypeStruct((SSM_HEADS, n), F32),
            jax.ShapeDtypeStruct((n, MLA_HEADS * HEAD_PAD), qdt),
            jax.ShapeDtypeStruct((n, KV_LORA), F32), jax.ShapeDtypeStruct((n, QK_ROPE), F32)]
    ospecs = [row(D_SSM), row(CONV_DIM), row(LANES), pl.BlockSpec((SSM_HEADS, tm), lambda i: (0, i)),
              row(MLA_HEADS * HEAD_PAD), row(KV_LORA), row(QK_ROPE)]
    if prompt:
        ins += [wts["wukr"], wts["e2"], wts["wuv"]]
        specs += [full(wts["wukr"]), full(wts["e2"]), full(wts["wuv"])]
        outs += [jax.ShapeDtypeStruct((n, MLA_HEADS * HEAD_PAD), BF16), jax.ShapeDtypeStruct((n, D_MLA), BF16)]
        ospecs += [row(MLA_HEADS * HEAD_PAD), row(D_MLA)]
    return pl.pallas_call(
        functools.partial(_in_kernel, prompt=prompt),
        grid=(n // tm,),
        in_specs=specs, out_specs=ospecs, out_shape=outs,
        compiler_params=_cparams(("arbitrary",)),
    )(*ins)


def _softplus(x):
    return jnp.maximum(x, 0.0) + jnp.log1p(jnp.exp(-jnp.abs(x)))


def _silu(x):
    return x / (1.0 + jnp.exp(-x))


def _ssd_kernel(z_ref, xbc_ref, misc_ref, dtt_ref, s0_ref, conv0_ref, cw_ref, cb_ref, dtbl_ref, dtbc_ref,
                al_ref, ac_ref, dsk_ref, gssm_ref, y_ref, sout_ref, state, prev, *, chunk):
    c = pl.program_id(1)
    nc = pl.num_programs(1)

    @pl.when(c == 0)
    def _():
        state[...] = s0_ref[...]
        prev[...] = conv0_ref[...]

    xr = xbc_ref[...]
    cat = jnp.concatenate([prev[...], xr], axis=0)
    acc = cb_ref[...] + cw_ref[CONV_WIDTH - 1:CONV_WIDTH, :] * xr
    for d in range(1, CONV_WIDTH):
        shifted = pltpu.roll(cat, d, axis=0)[SUBLANES:SUBLANES + chunk]
        acc = acc + cw_ref[CONV_WIDTH - 1 - d:CONV_WIDTH - d, :] * shifted
    prev[...] = xr[chunk - SUBLANES:chunk]
    xc = _silu(acc)
    xs = xc[:, :D_SSM]
    off_b = D_SSM
    off_c = D_SSM + SSM_GROUPS * SSM_STATE

    dtc = _softplus(misc_ref[...] + dtbl_ref[...])
    dtr = _softplus(dtt_ref[...] + dtbc_ref[...])
    row_i = lax.broadcasted_iota(I32, (chunk, chunk), 0)
    col_i = lax.broadcasted_iota(I32, (chunk, chunk), 1)
    causal = col_i <= row_i
    tril = causal.astype(F32)
    acum_c = _dot(tril, dtc * al_ref[...], HIGHEST)
    acum_r = _dot_nt(dtr * ac_ref[...], tril, HIGHEST)

    ys = []
    for g in range(SSM_GROUPS):
        bm = xc[:, off_b + g * SSM_STATE: off_b + (g + 1) * SSM_STATE].astype(BF16)
        cm = xc[:, off_c + g * SSM_STATE: off_c + (g + 1) * SSM_STATE].astype(BF16)
        cb = _dot_nt(cm, bm)
        for j in range(HEADS_PER_GROUP):
            hh = g * HEADS_PER_GROUP + j
            lane = MISC_DT + hh
            a_q = acum_c[:, lane:lane + 1]
            a_s = acum_r[hh:hh + 1, :]
            a_last = acum_c[chunk - 1:chunk, lane:lane + 1]
            lmat = jnp.exp(jnp.where(causal, a_q - a_s, NEG_BIG))
            x_j = xs[:, hh * SSM_HEAD_DIM:(hh + 1) * SSM_HEAD_DIM]
            xdt = x_j * dtc[:, lane:lane + 1]
            s_prev = state[hh]
            y = _dot((cb * lmat).astype(BF16), xdt.astype(BF16))
            y = y + _dot_nt(cm, s_prev.astype(BF16)) * jnp.exp(a_q)
            y = y + dsk_ref[:, hh * SSM_HEAD_DIM:(hh + 1) * SSM_HEAD_DIM] * x_j
            ys.append(y)
            xdd = (xdt * jnp.exp(a_last - a_q)).astype(BF16)
            state[hh] = jnp.exp(a_last) * s_prev + _dot_tn(xdd, bm)
    y = jnp.concatenate(ys, axis=1) * _silu(z_ref[...])
    y_ref[...] = _rms(y, gssm_ref[...])

    @pl.when(c == nc - 1)
    def _():
        sout_ref[...] = state[...]


def _ssd(z, xbc, misc, dtt3, s0, conv0, wts, *, nb, chunk):
    n = z.shape[0]
    nc = (n // nb) // chunk
    full = lambda a: pl.BlockSpec(a.shape, lambda b, c: (0,) * a.ndim)
    row = lambda w: pl.BlockSpec((chunk, w), lambda b, c: (b * nc + c, 0))
    names = ["conv_w", "conv_b", "dtb_lane", "dtb_col", "a_lane", "a_col", "dskip", "g_ssm"]
    st_spec = pl.BlockSpec((None, SSM_HEADS, SSM_HEAD_DIM, SSM_STATE), lambda b, c: (b, 0, 0, 0))
    return pl.pallas_call(
        functools.partial(_ssd_kernel, chunk=chunk),
        grid=(nb, nc),
        in_specs=[row(D_SSM), row(CONV_DIM), row(LANES),
                  pl.BlockSpec((None, SSM_HEADS, chunk), lambda b, c: (b * nc + c, 0, 0)),
                  st_spec,
                  pl.BlockSpec((None, SUBLANES, CONV_DIM), lambda b, c: (b, 0, 0))]
                 + [full(wts[k]) for k in names],
        out_specs=[row(D_SSM), st_spec],
        out_shape=[jax.ShapeDtypeStruct((n, D_SSM), F32),
                   jax.ShapeDtypeStruct((nb, SSM_HEADS, SSM_HEAD_DIM, SSM_STATE), F32)],
        scratch_shapes=[pltpu.VMEM((SSM_HEADS, SSM_HEAD_DIM, SSM_STATE), F32),
                        pltpu.VMEM((SUBLANES, CONV_DIM), F32)],
        compiler_params=_cparams(("arbitrary", "arbitrary")),
    )(z, xbc, misc, dtt3, s0, conv0, *[wts[k] for k in names])


def _attn_kernel(q_ref, k_ref, v_ref, o_ref, m_sc, l_sc, acc_sc, *, tq, tk):
    i = pl.program_id(1)
    j = pl.program_id(2)

    @pl.when(j == 0)
    def _():
        m_sc[...] = jnp.full(m_sc.shape, NEG_BIG, F32)
        l_sc[...] = jnp.zeros(l_sc.shape, F32)
        acc_sc[...] = jnp.zeros(acc_sc.shape, F32)

    @pl.when(j <= i)
    def _():
        qpos = i * tq + lax.broadcasted_iota(I32, (tq, tk), 0)
        kpos = j * tk + lax.broadcasted_iota(I32, (tq, tk), 1)
        keep = kpos <= qpos
        for h in range(MLA_HEADS):
            q_h = q_ref[:, h * HEAD_PAD:(h + 1) * HEAD_PAD]
            k_h = k_ref[:, h * HEAD_PAD:(h + 1) * HEAD_PAD]
            s = jnp.where(keep, _dot_nt(q_h, k_h), NEG_BIG)
            m_prev = m_sc[h]
            m_new = jnp.maximum(m_prev, jnp.max(s, axis=-1, keepdims=True))
            alpha = jnp.exp(m_prev - m_new)
            p = jnp.exp(s - m_new)
            l_sc[h] = alpha * l_sc[h] + jnp.sum(p, axis=-1, keepdims=True)
            m_sc[h] = m_new
            v_pair = v_ref[:, (h // 2) * LANES:(h // 2 + 1) * LANES]
            acc_sc[h] = alpha * acc_sc[h] + _dot(p.astype(BF16), v_pair)

    @pl.when(j == i)
    def _():
        lane = lax.broadcasted_iota(I32, (tq, LANES), 1)
        for hp in range(MLA_HEADS // 2):
            o_even = acc_sc[2 * hp] / l_sc[2 * hp]
            o_odd = acc_sc[2 * hp + 1] / l_sc[2 * hp + 1]
            o_ref[:, hp * LANES:(hp + 1) * LANES] = jnp.where(lane < V_HEAD, o_even, o_odd)


def _attn_prompt(q, kf, v, *, nb, tq, tk):
    n = q.shape[0]
    t = n // nb
    nq, nk = t // tq, t // tk
    return pl.pallas_call(
        functools.partial(_attn_kernel, tq=tq, tk=tk),
        grid=(nb, nq, nk),
        in_specs=[pl.BlockSpec((tq, MLA_HEADS * HEAD_PAD), lambda b, i, j: (b * nq + i, 0)),
                  pl.BlockSpec((tk, MLA_HEADS * HEAD_PAD), lambda b, i, j: (b * nk + jnp.minimum(j, i), 0)),
                  pl.BlockSpec((tk, D_MLA), lambda b, i, j: (b * nk + jnp.minimum(j, i), 0))],
        out_specs=pl.BlockSpec((tq, D_MLA), lambda b, i, j: (b * nq + i, 0)),
        out_shape=jax.ShapeDtypeStruct((n, D_MLA), F32),
        scratch_shapes=[pltpu.VMEM((MLA_HEADS, tq, 1), F32), pltpu.VMEM((MLA_HEADS, tq, 1), F32),
                        pltpu.VMEM((MLA_HEADS, tq, LANES), F32)],
        compiler_params=_cparams(("arbitrary", "arbitrary", "arbitrary")),
    )(q, kf, v)


def _attn_s_kernel(*refs, pg, td):
    pt_ref = refs[0]
    q_ref, latn_ref, krn_ref, wabs_ref, rsel_ref, wuv_ref = refs[1:7]
    lat_refs = refs[7:7 + pg]
    rope_refs = refs[7 + pg:7 + 2 * pg]
    o_ref = refs[7 + 2 * pg]
    ql_sc, qr_sc, m_sc, l_sc, acc_sc = refs[8 + 2 * pg:]
    del pt_ref
    g = pl.program_id(1)
    ng = pl.num_programs(1)
    rows = MLA_HEADS * td

    def update(s, vals):
        m_prev = m_sc[...]
        m_new = jnp.maximum(m_prev, jnp.max(s, axis=-1, keepdims=True))
        alpha = jnp.exp(m_prev - m_new)
        p = jnp.exp(s - m_new)
        l_sc[...] = alpha * l_sc[...] + jnp.sum(p, axis=-1, keepdims=True)
        m_sc[...] = m_new
        acc_sc[...] = alpha * acc_sc[...] + _dot(p.astype(BF16), vals)

    @pl.when(g == 0)
    def _():
        for h in range(MLA_HEADS):
            q_h = q_ref[:, h * HEAD_PAD:(h + 1) * HEAD_PAD].astype(BF16)
            ql_sc[h * td:(h + 1) * td, :] = _dot(q_h, wabs_ref[h])
            qr_sc[h * td:(h + 1) * td, :] = _dot(q_h, rsel_ref[...])
        m_sc[...] = jnp.full(m_sc.shape, NEG_BIG, F32)
        l_sc[...] = jnp.zeros(l_sc.shape, F32)
        acc_sc[...] = jnp.zeros(acc_sc.shape, F32)
        latn = latn_ref[...].astype(BF16)
        s = _dot_nt(ql_sc[...].astype(BF16), latn) + _dot_nt(qr_sc[...].astype(BF16), krn_ref[...].astype(BF16))
        tok = lax.rem(lax.broadcasted_iota(I32, (rows, td), 0), td)
        key = lax.broadcasted_iota(I32, (rows, td), 1)
        update(jnp.where(key <= tok, s, NEG_BIG), latn)

    lat = jnp.concatenate([r[...].astype(BF16) for r in lat_refs], axis=0)
    rope = jnp.concatenate([r[...].astype(BF16) for r in rope_refs], axis=0)
    s = _dot_nt(ql_sc[...].astype(BF16), lat) + _dot_nt(qr_sc[...].astype(BF16), rope)
    update(s, lat)

    @pl.when(g == ng - 1)
    def _():
        o_lat = (acc_sc[...] / l_sc[...]).astype(BF16)
        outs = [_dot(o_lat[h * td:(h + 1) * td, :], wuv_ref[h]) for h in range(MLA_HEADS)]
        o_ref[...] = jnp.concatenate(outs, axis=1)


def _attn_sample(q, lat_new, kr_new, cache_lat, cache_rope, page_table, wts, *, td, pg):
    n = q.shape[0]
    nb = n // td
    n_pages = page_table.shape[1]
    page = cache_lat.shape[1]
    ng = n_pages // pg
    rows = MLA_HEADS * td
    full = lambda a: pl.BlockSpec(a.shape, lambda b, g, pt: (0,) * a.ndim)
    row = lambda w: pl.BlockSpec((td, w), lambda b, g, pt: (b, 0))

    def page_spec(width, k):
        return pl.BlockSpec((None, page, width), lambda b, g, pt: (pt[b, g * pg + k], 0, 0))

    grid_spec = pltpu.PrefetchScalarGridSpec(
        num_scalar_prefetch=1,
        grid=(nb, ng),
        in_specs=[row(MLA_HEADS * HEAD_PAD), row(KV_LORA), row(QK_ROPE),
                  full(wts["wabs"]), full(wts["rsel"]), full(wts["wuv_h"])]
                 + [page_spec(KV_LORA, k) for k in range(pg)]
                 + [page_spec(QK_ROPE, k) for k in range(pg)],
        out_specs=pl.BlockSpec((td, D_MLA), lambda b, g, pt: (b, 0)),
        scratch_shapes=[pltpu.VMEM((rows, KV_LORA), F32), pltpu.VMEM((rows, QK_ROPE), F32),
                        pltpu.VMEM((rows, 1), F32), pltpu.VMEM((rows, 1), F32),
                        pltpu.VMEM((rows, KV_LORA), F32)])
    return pl.pallas_call(
        functools.partial(_attn_s_kernel, pg=pg, td=td),
        grid_spec=grid_spec,
        out_shape=jax.ShapeDtypeStruct((n, D_MLA), F32),
        compiler_params=_cparams(("arbitrary", "arbitrary")),
    )(page_table, q, lat_new, kr_new, wts["wabs"], wts["rsel"], wts["wuv_h"],
      *([cache_lat] * pg), *([cache_rope] * pg))


def _out_kernel(ys_ref, ym_ref, x_ref, g1_ref, sh_ref, sc_ref, gffn_ref, woa_ref, wob_ref, wr_ref, br_ref,
                x1_ref, h2_ref, tope_ref, gate_ref):
    mix = _dot(ys_ref[...].astype(BF16), woa_ref[...]) + _dot(ym_ref[...].astype(BF16), wob_ref[...])
    x1 = x_ref[...] + g1_ref[...] * mix
    x1_ref[...] = x1
    h2 = _rms(x1, gffn_ref[...]) * (1.0 + sc_ref[...]) + sh_ref[...]
    h2_ref[...] = h2
    work = _dot(h2, wr_ref[...], HIGHEST) + br_ref[...]
    lane = lax.broadcasted_iota(I32, work.shape, 1)
    vals, idxs = [], []
    for _ in range(TOP_K):
        m = jnp.max(work, axis=-1, keepdims=True)
        idx = jnp.min(jnp.where(work == m, lane, LANES), axis=-1, keepdims=True)
        vals.append(m)
        idxs.append(idx)
        work = jnp.where(lane == idx, -jnp.inf, work)
    exps = [jnp.exp(v - vals[0]) for v in vals]
    denom = exps[0] + exps[1] + exps[2] + exps[3]
    tope = jnp.zeros(work.shape, I32)
    gate = jnp.zeros(work.shape, F32)
    for k in range(TOP_K):
        tope = jnp.where(lane == k, idxs[k], tope)
        gate = jnp.where(lane == k, exps[k] / denom, gate)
    tope_ref[...] = tope
    gate_ref[...] = gate


def _out_proj(yssm, ymla, x2d, g1, sh2, sc2, wts, *, tm):
    n = x2d.shape[0]
    g, r, _ = g1.shape
    tiles_per_group = (n // g) // tm
    full = lambda a: pl.BlockSpec(a.shape, lambda i: (0,) * a.ndim)
    row = lambda w: pl.BlockSpec((tm, w), lambda i: (i, 0))
    mod = pl.BlockSpec((None, r, D_MODEL), lambda i: (i // tiles_per_group, 0, 0))
    names = ["g_ffn", "wo_a", "wo_b", "w_router", "b_router"]
    return pl.pallas_call(
        _out_kernel,
        grid=(n // tm,),
        in_specs=[row(D_SSM), row(D_MLA), row(D_MODEL), mod, mod, mod] + [full(wts[k]) for k in names],
        out_specs=[row(D_MODEL), row(D_MODEL), row(LANES), row(LANES)],
        out_shape=[jax.ShapeDtypeStruct((n, D_MODEL), F32), jax.ShapeDtypeStruct((n, D_MODEL), F32),
                   jax.ShapeDtypeStruct((n, LANES), I32), jax.ShapeDtypeStruct((n, LANES), F32)],
        compiler_params=_cparams(("arbitrary",)),
    )(yssm, ymla, x2d, g1, sh2, sc2, *[wts[k] for k in names])


def _rank_kernel(tope_ref, rank_ref, cnt_ref, carry, *, tr):
    i = pl.program_id(0)

    @pl.when(i == 0)
    def _():
        carry[...] = jnp.zeros(carry.shape, F32)

    tope = tope_ref[...]
    lane = lax.broadcasted_iota(I32, (tr, LANES), 1)
    sel = jnp.zeros((tr, LANES), F32)
    for k in range(TOP_K):
        sel = jnp.where(lane == tope[:, k:k + 1], 1.0, sel)
    r_i = lax.broadcasted_iota(I32, (tr, tr), 0)
    c_i = lax.broadcasted_iota(I32, (tr, tr), 1)
    before = (c_i < r_i).astype(BF16)
    rank = carry[0:1, :] + _dot(before, sel.astype(BF16))
    out = jnp.zeros((tr, LANES), F32)
    for k in range(TOP_K):
        r_k = jnp.sum(jnp.where(lane == tope[:, k:k + 1], rank, 0.0), axis=-1, keepdims=True)
        out = jnp.where(lane == k, r_k, out)
    rank_ref[...] = out.astype(I32)
    carry[...] = carry[...] + jnp.sum(sel, axis=0, keepdims=True)
    cnt_ref[...] = carry[...].astype(I32)


def _ranks(tope, *, tr):
    n = tope.shape[0]
    return pl.pallas_call(
        functools.partial(_rank_kernel, tr=tr),
        grid=(n // tr,),
        in_specs=[pl.BlockSpec((tr, LANES), lambda i: (i, 0))],
        out_specs=[pl.BlockSpec((tr, LANES), lambda i: (i, 0)),
                   pl.BlockSpec((SUBLANES, LANES), lambda i: (0, 0))],
        out_shape=[jax.ShapeDtypeStruct((n, LANES), I32), jax.ShapeDtypeStruct((SUBLANES, LANES), I32)],
        scratch_shapes=[pltpu.VMEM((SUBLANES, LANES), F32)],
        compiler_params=_cparams(("arbitrary",)),
    )(tope)


def _row_copy(src_ref, dst_ref, sem, src_row, dst_row):
    return pltpu.make_async_copy(src_ref.at[pl.ds(src_row, 1), :], dst_ref.at[pl.ds(dst_row, 1), :], sem)


def _dispatch_kernel(dest_ref, h_ref, xs_in_ref, xs_ref, sem, *, td):
    del xs_in_ref

    def issue(r, carry):
        for k in range(TOP_K):
            _row_copy(h_ref, xs_ref, sem, r, dest_ref[r * TOP_K + k]).start()
        return carry

    lax.fori_loop(0, td, issue, 0)

    def drain(r, carry):
        for k in range(TOP_K):
            _row_copy(h_ref, xs_ref, sem, 0, 0).wait()
        return carry

    lax.fori_loop(0, td, drain, 0)


def _dispatch(h2, dest_flat, cap, *, td):
    n = h2.shape[0]
    xs0 = jnp.zeros((cap, D_MODEL), F32)
    return pl.pallas_call(
        functools.partial(_dispatch_kernel, td=td),
        grid=(n // td,),
        in_specs=[pl.BlockSpec((td * TOP_K,), lambda i: (i,), memory_space=pltpu.SMEM),
                  pl.BlockSpec((td, D_MODEL), lambda i: (i, 0)),
                  pl.BlockSpec(memory_space=pl.ANY)],
        out_specs=pl.BlockSpec(memory_space=pl.ANY),
        out_shape=jax.ShapeDtypeStruct((cap, D_MODEL), F32),
        scratch_shapes=[pltpu.SemaphoreType.DMA(())],
        input_output_aliases={2: 0},
        compiler_params=_cparams(("arbitrary",)),
    )(dest_flat, h2, xs0)


def _moe_kernel(be_ref, nu_ref, xs_ref, wgu_ref, bgu_ref, wd_ref, bd_ref, y_ref, wgu_b, wd_b):
    i = pl.program_id(0)
    active = i < nu_ref[0]
    prev_e = be_ref[jnp.maximum(i - 1, 0)]
    fresh = jnp.logical_or(i == 0, be_ref[i] != prev_e)

    @pl.when(jnp.logical_and(active, fresh))
    def _():
        wgu_b[...] = wgu_ref[...].astype(BF16)
        wd_b[...] = wd_ref[...].astype(BF16)

    @pl.when(active)
    def _():
        gu = _dot(xs_ref[...].astype(BF16), wgu_b[...]) + bgu_ref[...]
        gate = jnp.minimum(gu[:, :D_EXPERT], SWIGLU_LIMIT)
        up = jnp.clip(gu[:, D_EXPERT:], -SWIGLU_LIMIT, SWIGLU_LIMIT)
        act = (up + 1.0) * (gate / (1.0 + jnp.exp(-SWIGLU_ALPHA * gate)))
        y_ref[...] = _dot(act.astype(BF16), wd_b[...]) + bd_ref[...]

    @pl.when(jnp.logical_not(active))
    def _():
        y_ref[...] = jnp.zeros(y_ref.shape, F32)


def _moe(xs, block_e, n_used, w_gu, b_gu, w_d, b_d, *, blk):
    cap = xs.shape[0]
    n_blocks = cap // blk

    def xmap(i, be, nu):
        return (jnp.minimum(i, nu[0] - 1), 0)

    def wmap(i, be, nu):
        return (be[jnp.minimum(i, nu[0] - 1)], 0, 0)

    grid_spec = pltpu.PrefetchScalarGridSpec(
        num_scalar_prefetch=2,
        grid=(n_blocks,),
        in_specs=[pl.BlockSpec((blk, D_MODEL), xmap),
                  pl.BlockSpec((None, D_MODEL, 2 * D_EXPERT), wmap),
                  pl.BlockSpec((None, 1, 2 * D_EXPERT), wmap),
                  pl.BlockSpec((None, D_EXPERT, D_MODEL), wmap),
                  pl.BlockSpec((None, 1, D_MODEL), wmap)],
        out_specs=pl.BlockSpec((blk, D_MODEL), lambda i, be, nu: (i, 0)),
        scratch_shapes=[pltpu.VMEM((D_MODEL, 2 * D_EXPERT), BF16), pltpu.VMEM((D_EXPERT, D_MODEL), BF16)])
    return pl.pallas_call(
        _moe_kernel,
        grid_spec=grid_spec,
        out_shape=jax.ShapeDtypeStruct((cap, D_MODEL), F32),
        compiler_params=_cparams(("arbitrary",)),
    )(block_e, n_used, xs, w_gu, b_gu.reshape(N_EXPERTS, 1, 2 * D_EXPERT), w_d, b_d.reshape(N_EXPERTS, 1, D_MODEL))


def _combine_kernel(dest_ref, gate_ref, x1_ref, g2_ref, gfin_ref, yb_ref, o_ref, buf, sem, *, tc):
    def issue(r, carry):
        for k in range(TOP_K):
            _row_copy(yb_ref, buf.at[k], sem, dest_ref[r * TOP_K + k], r).start()
        return carry

    lax.fori_loop(0, tc, issue, 0)

    def drain(r, carry):
        for k in range(TOP_K):
            _row_copy(yb_ref, buf.at[k], sem, 0, 0).wait()
        return carry

    lax.fori_loop(0, tc, drain, 0)
    gates = gate_ref[...]
    moe = jnp.zeros((tc, D_MODEL), F32)
    for k in range(TOP_K):
        moe = moe + gates[:, k:k + 1] * buf[k]
    x2 = x1_ref[...] + g2_ref[...] * moe
    o_ref[...] = _rms(x2, gfin_ref[...])


def _combine(dest_flat, gates, x1, g2, g_final, yb, *, tc, rows_per_group):
    n = x1.shape[0]
    g, r, _ = g2.shape
    tiles_per_group = rows_per_group // tc
    return pl.pallas_call(
        functools.partial(_combine_kernel, tc=tc),
        grid=(n // tc,),
        in_specs=[pl.BlockSpec((tc * TOP_K,), lambda i: (i,), memory_space=pltpu.SMEM),
                  pl.BlockSpec((tc, LANES), lambda i: (i, 0)),
                  pl.BlockSpec((tc, D_MODEL), lambda i: (i, 0)),
                  pl.BlockSpec((None, r, D_MODEL), lambda i: (i // tiles_per_group, 0, 0)),
                  pl.BlockSpec((1, D_MODEL), lambda i: (0, 0)),
                  pl.BlockSpec(memory_space=pl.ANY)],
        out_specs=pl.BlockSpec((tc, D_MODEL), lambda i: (i, 0)),
        out_shape=jax.ShapeDtypeStruct((n, D_MODEL), F32),
        scratch_shapes=[pltpu.VMEM((TOP_K, tc, D_MODEL), F32), pltpu.SemaphoreType.DMA(())],
        compiler_params=_cparams(("arbitrary",)),
    )(dest_flat, gates, x1, g2, g_final.reshape(1, D_MODEL), yb)


def _prep_weights(w_in, conv_w, conv_b, dt_bias, a_log, d_skip, g_ssm, g_mix, g_ffn, g_q, w_uq, g_kv, w_uk, w_uv,
                  w_o, w_router, b_router):
    s1 = D_SSM
    s2 = s1 + CONV_DIM
    s3 = s2 + SSM_HEADS
    s4 = s3 + Q_LORA
    s5 = s4 + KV_LORA
    half = QK_ROPE // 2
    dt_w = w_in[:, s2:s3]
    kr_w = w_in[:, s5:]
    kr_sw = jnp.concatenate([kr_w[:, half:], kr_w[:, :half]], axis=1)
    misc_w = jnp.concatenate([kr_w, kr_sw, dt_w, jnp.zeros((D_MODEL, LANES - 2 * QK_ROPE - SSM_HEADS), F32)], axis=1)
    win = jnp.concatenate([w_in[:, :s2], w_in[:, s3:s5], misc_w], axis=1).astype(BF16)

    rope_w = w_uq[..., QK_NOPE:]
    rope_sw = jnp.concatenate([rope_w[..., half:], rope_w[..., :half]], axis=-1)
    pad_tail = jnp.zeros((Q_LORA, MLA_HEADS, HEAD_PAD - QK_NOPE - QK_ROPE), F32)
    wq = jnp.concatenate([w_uq, pad_tail], axis=-1).reshape(Q_LORA, MLA_HEADS * HEAD_PAD).astype(BF16)
    wqs = jnp.concatenate([jnp.zeros((Q_LORA, MLA_HEADS, QK_NOPE), F32), rope_sw, pad_tail], axis=-1)
    wqs = wqs.reshape(Q_LORA, MLA_HEADS * HEAD_PAD).astype(BF16)

    wukr = jnp.concatenate([w_uk, jnp.zeros((KV_LORA, MLA_HEADS, HEAD_PAD - QK_NOPE), F32)], axis=-1)
    wukr = wukr.reshape(KV_LORA, MLA_HEADS * HEAD_PAD).astype(BF16)
    rr = jnp.arange(LANES)[:, None]
    cc = jnp.arange(MLA_HEADS * HEAD_PAD)[None, :]
    e2 = jnp.logical_and(rr < QK_ROPE, (cc % HEAD_PAD) == QK_NOPE + rr).astype(BF16)
    rsel = (jnp.arange(HEAD_PAD)[:, None] == QK_NOPE + jnp.arange(QK_ROPE)[None, :]).astype(BF16)
    wabs = jnp.transpose(w_uk, (1, 2, 0))
    wabs = jnp.concatenate([wabs, jnp.zeros((MLA_HEADS, HEAD_PAD - QK_NOPE, KV_LORA), F32)], axis=1).astype(BF16)

    lane_pad = lambda v: jnp.zeros((1, LANES), F32).at[0, MISC_DT:MISC_DT + SSM_HEADS].set(v)
    a_neg = -jnp.exp(a_log)
    return {
        "g_mix": g_mix.reshape(1, -1), "win": win, "wdt": dt_w.T.astype(BF16), "g_q": g_q.reshape(1, -1),
        "wq": wq, "wqs": wqs, "g_kv": g_kv.reshape(1, -1), "wukr": wukr, "e2": e2,
        "wuv": w_uv.reshape(KV_LORA, D_MLA).astype(BF16),
        "wabs": wabs, "rsel": rsel, "wuv_h": jnp.transpose(w_uv, (1, 0, 2)).astype(BF16),
        "conv_w": conv_w, "conv_b": conv_b.reshape(1, -1),
        "dtb_lane": lane_pad(dt_bias), "dtb_col": dt_bias.reshape(-1, 1),
        "a_lane": lane_pad(a_neg), "a_col": a_neg.reshape(-1, 1),
        "dskip": jnp.repeat(d_skip, SSM_HEAD_DIM).reshape(1, -1), "g_ssm": g_ssm.reshape(1, -1),
        "g_ffn": g_ffn.reshape(1, -1), "wo_a": w_o[:D_SSM].astype(BF16), "wo_b": w_o[D_SSM:].astype(BF16),
        "w_router": jnp.concatenate([w_router, jnp.zeros((D_MODEL, LANES - N_EXPERTS), F32)], axis=1),
        "b_router": jnp.concatenate([b_router, jnp.full((LANES - N_EXPERTS,), NEG_BIG, F32)]).reshape(1, -1),
    }


def _rope_tables(pos):
    half = QK_ROPE // 2
    inv = ROPE_THETA ** (-jnp.arange(0, QK_ROPE, 2, dtype=F32) / QK_ROPE)
    ang = pos.astype(F32)[:, None] * inv[None, :]
    cos, sin = jnp.cos(ang), jnp.sin(ang)
    t = pos.shape[0]
    ones = jnp.ones((t, QK_NOPE), F32)
    zq = jnp.zeros((t, HEAD_PAD - QK_NOPE - QK_ROPE), F32)
    cosq = jnp.concatenate([ones, cos, cos, zq], axis=1) * ATTN_SCALE
    sinq = jnp.concatenate([jnp.zeros((t, QK_NOPE), F32), -sin, sin, zq], axis=1) * ATTN_SCALE
    tabk = jnp.concatenate([cos, cos, -sin, sin, jnp.zeros((t, LANES - 2 * QK_ROPE), F32)], axis=1)
    del half
    return cosq, sinq, tabk


def _pick(n, prefs):
    for p in prefs:
        if n % p == 0:
            return p
    return n


def _mixer(x, ada, pos, s0, conv0, wts, *, prompt, cache=None):
    nb, t, _ = x.shape
    n = nb * t
    x2d = x.reshape(n, D_MODEL)
    sh1, sc1, g1, sh2, sc2, g2 = [ada[:, k * D_MODEL:(k + 1) * D_MODEL] for k in range(N_ADA)]
    tm = _pick(n, (256, 128, 64, 32, 16, 8))
    if prompt:
        assert t % tm == 0
        shape_mod = lambda a: a.reshape(nb, 1, D_MODEL)
        tabs = _rope_tables(pos)
    else:
        assert tm % t == 0
        shape_mod = lambda a: jnp.repeat(a, t, axis=0).reshape(n // tm, tm, D_MODEL)
        tabs = tuple(jnp.tile(a, (tm // t, 1)) for a in _rope_tables(pos))
    mods = [shape_mod(a) for a in (sh1, sc1, g1, sh2, sc2, g2)]
    outs = _in_proj(x2d, mods[0], mods[1], tabs, wts, prompt=prompt, tm=tm)
    z, xbc, misc, dtt, q, lat, kr = outs[:7]

    chunk = min(SSD_CHUNK, t)
    assert t % chunk == 0
    nct = n // chunk
    dtt3 = jnp.transpose(dtt.reshape(SSM_HEADS, nct, chunk), (1, 0, 2))
    yssm, s_new = _ssd(z, xbc, misc, dtt3, s0, conv0, wts, nb=nb, chunk=chunk)
    conv_new = xbc.reshape(nb, t, CONV_DIM)[:, t - (CONV_WIDTH - 1):]

    if prompt:
        kf, v = outs[7:]
        tq = _pick(t, (512, 256, 128))
        ymla = _attn_prompt(q, kf, v, nb=nb, tq=tq, tk=tq)
    else:
        cache_lat, cache_rope, page_table = cache
        pg = _pick(page_table.shape[1], (8, 4, 2, 1))
        ymla = _attn_sample(q, lat, kr, cache_lat, cache_rope, page_table, wts, td=t, pg=pg)

    x1, h2, tope, gates = _out_proj(yssm, ymla, x2d, mods[2], mods[3], mods[4], wts, tm=tm)
    state = (lat.reshape(nb, t, KV_LORA), kr.reshape(nb, t, QK_ROPE), s_new, conv_new)
    return x1, h2, tope, gates, mods[5], state


def kernel(x_prompt, x_sample, c_prompt, c_sample, cache_kv_latent, cache_k_rope, page_table, state_ssm, state_conv, w_ada, b_ada, g_mix, g_ffn, w_in, conv_w, conv_b, dt_bias, a_log, d_skip, g_ssm, g_q, w_uq, g_kv, w_uk, w_uv, w_o, w_router, b_router, w_gate_up, b_gate_up, w_down, b_down, g_final):
    depth = w_ada.shape[0]
    assert depth == 1
    bp, tp = x_prompt.shape[0], x_prompt.shape[1]
    bd, td = x_sample.shape[0], x_sample.shape[1]
    past = page_table.shape[1] * cache_kv_latent.shape[2]
    l = 0
    wts = _prep_weights(w_in[l], conv_w[l], conv_b[l], dt_bias[l], a_log[l], d_skip[l], g_ssm[l], g_mix[l], g_ffn[l],
                        g_q[l], w_uq[l], g_kv[l], w_uk[l], w_uv[l], w_o[l], w_router[l], b_router[l])

    nc_all = bp + bd
    c_rows = -(-nc_all // SUBLANES) * SUBLANES
    c_all = jnp.concatenate([c_prompt, c_sample, jnp.zeros((c_rows - nc_all, D_MODEL), F32)], axis=0)
    ada = _ada(c_all, w_ada[l], b_ada[l])

    s0_p = jnp.zeros((bp, SSM_HEADS, SSM_HEAD_DIM, SSM_STATE), F32)
    conv0_p = jnp.zeros((bp, SUBLANES, CONV_DIM), F32)
    conv0_s = jnp.concatenate([jnp.zeros((bd, SUBLANES - (CONV_WIDTH - 1), CONV_DIM), F32), state_conv[l]], axis=1)

    x1p, h2p, tope_p, gate_p, g2p, st_p = _mixer(
        x_prompt, ada[:bp], jnp.arange(tp), s0_p, conv0_p, wts, prompt=True)
    x1s, h2s, tope_s, gate_s, g2s, st_s = _mixer(
        x_sample, ada[bp:bp + bd], past + jnp.arange(td), state_ssm[l], conv0_s, wts, prompt=False,
        cache=(cache_kv_latent[l], cache_k_rope[l], page_table))

    n_p, n_s = bp * tp, bd * td
    n = n_p + n_s
    h2 = jnp.concatenate([h2p, h2s], axis=0)
    tope = jnp.concatenate([tope_p, tope_s], axis=0)
    tr = _pick(n, (256, 128, 64, 32, 16, 8))
    rank, counts = _ranks(tope, tr=tr)
    counts = counts[0, :N_EXPERTS]
    blk = 256
    nblk_e = (counts + blk - 1) // blk
    blk_end = jnp.cumsum(nblk_e)
    pad_start = (blk_end - nblk_e) * blk
    n_blocks = -(-(n * TOP_K) // blk) + N_EXPERTS
    tope4 = tope[:, :TOP_K]
    dest = (pad_start[tope4] + rank[:, :TOP_K]).astype(I32).reshape(-1)
    block_e = jnp.sum(blk_end[None, :] <= jnp.arange(n_blocks)[:, None], axis=1)
    block_e = jnp.minimum(block_e, N_EXPERTS - 1).astype(I32)
    n_used = blk_end[-1:].astype(I32)

    td_rows = _pick(n, (256, 128, 64, 32, 16, 8))
    xs = _dispatch(h2, dest, n_blocks * blk, td=td_rows)
    yb = _moe(xs, block_e, n_used, w_gate_up[l], b_gate_up[l], w_down[l], b_down[l], blk=blk)

    tc_p = _pick(tp, (128, 64, 32, 16, 8))
    y_p = _combine(dest[:n_p * TOP_K], gate_p, x1p, g2p, g_final, yb, tc=tc_p, rows_per_group=tp)
    tc_s = g2s.shape[1]
    y_s = _combine(dest[n_p * TOP_K:], gate_s, x1s, g2s, g_final, yb, tc=tc_s, rows_per_group=tc_s)

    lat_p, kr_p, ssm_p, conv_p = st_p
    lat_s, kr_s, ssm_s, conv_s = st_s
    stack = lambda a: a[None]
    return (y_p.reshape(bp, tp, D_MODEL), y_s.reshape(bd, td, D_MODEL),
            stack(lat_p), stack(kr_p), stack(ssm_p.astype(state_ssm.dtype)), stack(conv_p),
            stack(lat_s), stack(kr_s), stack(ssm_s.astype(state_ssm.dtype)), stack(conv_s))
```
